```python
import jax, jax.numpy as jnp
from jax import lax
import numpy as np

D_MODEL = 4096
BATCH = 1
SEQ = 8192
DEPTH = 1

MIX_WIDTH = D_MODEL
HEAD_DIM = 128
SB_WIDTH = MIX_WIDTH // 2
SB_HEADS = SB_WIDTH // HEAD_DIM
SGU_WIDTH = MIX_WIDTH - SB_WIDTH
SGU_GROUP = 128
SGU_GROUPS = SGU_WIDTH // SGU_GROUP
CHUNK = 128
Q_BLOCK = 128
IN_COLS = 3 * SB_WIDTH + 2 * SGU_WIDTH
PEER_HEADS = 8
PEER_NKEYS = 128
PEER_EXPERTS = PEER_NKEYS * PEER_NKEYS
PEER_QDIM = 512
PEER_HALF = PEER_QDIM // 2
PEER_TOPK = 16
PEER_TOK_CHUNK = 64
EPS = 1e-6

kernel_name = "hymba_style_stickbreak_sgu_peer"


def rms_norm(x, g):
    xf = x.astype(jnp.float32)
    y = xf * lax.rsqrt(jnp.mean(xf * xf, axis=-1, keepdims=True) + EPS)
    return (y * g.astype(jnp.float32)).astype(x.dtype)


def stick_breaking_attention(q, k, v):
    B, H, S, Dh = q.shape
    nb = S // Q_BLOCK
    qb = q.reshape(B, H, nb, Q_BLOCK, Dh).transpose(2, 0, 1, 3, 4)
    starts = jnp.arange(nb, dtype=jnp.int32) * Q_BLOCK
    kpos = jnp.arange(S, dtype=jnp.int32)
    kf = k.astype(jnp.float32)
    vf = v.astype(jnp.float32)
    scale = HEAD_DIM ** -0.5

    def block(args):
        qblk, start = args
        z = jnp.einsum('bhqd,bhkd->bhqk', qblk.astype(jnp.float32), kf) * scale
        qpos = start + jnp.arange(Q_BLOCK, dtype=jnp.int32)
        mask = kpos[None, :] < qpos[:, None]
        log_1m = jnp.where(mask, jax.nn.log_sigmoid(-z), 0.0)
        suffix = lax.cumsum(log_1m, axis=3, reverse=True) - log_1m
        a = jnp.where(mask, jnp.exp(jax.nn.log_sigmoid(z) + suffix), 0.0)
        return jnp.einsum('bhqk,bhkd->bhqd', a, vf)

    out = lax.map(block, (qb, starts))
    return out.transpose(1, 2, 0, 3, 4).reshape(B, H, S, Dh)


def spatial_gating(u, v, v_norm_g, w_s, b_s):
    B, S, _ = u.shape
    nc = S // CHUNK
    v5 = v.reshape(B, nc, CHUNK, SGU_GROUPS, SGU_GROUP)
    vn = rms_norm(v5, v_norm_g.reshape(SGU_GROUPS, SGU_GROUP))
    w_causal = jnp.tril(w_s)
    mixed = jnp.einsum('gts,bnsgc->bntgc', w_causal, vn) + b_s.T[:, :, None]
    out = u.reshape(B, nc, CHUNK, SGU_GROUPS, SGU_GROUP) * mixed
    return out.reshape(B, S, SGU_WIDTH).astype(u.dtype)


def peer_ffn(h, w_q, sub_keys, u_tab, v_tab):
    B, S, D = h.shape
    T = B * S
    K = PEER_TOPK
    t = h.reshape(T, D)
    q = (t @ w_q).reshape(T, PEER_HEADS, 2, PEER_HALF)
    scores = jnp.einsum('thpd,hpnd->thpn', q, sub_keys).astype(jnp.float32)
    s_top, i_top = lax.top_k(scores, K)
    cand = (s_top[:, :, 0, :, None] + s_top[:, :, 1, None, :]).reshape(T, PEER_HEADS, K * K)
    best, flat = lax.top_k(cand, K)
    i1 = jnp.take_along_axis(i_top[:, :, 0], flat // K, axis=-1)
    i2 = jnp.take_along_axis(i_top[:, :, 1], flat % K, axis=-1)
    expert = (i1 * PEER_NKEYS + i2).reshape(T, PEER_HEADS * K)
    gate = jax.nn.softmax(best, axis=-1).reshape(T, PEER_HEADS * K)
    nt = T // PEER_TOK_CHUNK

    def chunk(args):
        tc, ec, gc = args
        uc = u_tab[ec]
        act = jax.nn.gelu(jnp.einsum('cd,ced->ce', tc, uc).astype(jnp.float32))
        coef = (gc * act).astype(tc.dtype)
        return jnp.einsum('ce,ced->cd', coef, v_tab[ec])

    out = lax.map(chunk, (t.reshape(nt, PEER_TOK_CHUNK, D),
                          expert.reshape(nt, PEER_TOK_CHUNK, PEER_HEADS * K),
                          gate.reshape(nt, PEER_TOK_CHUNK, PEER_HEADS * K)))
    return out.reshape(B, S, D).astype(h.dtype)


def setup_inputs(seed: int = 0) -> dict:
    key = jax.random.key(seed)
    ks = jax.random.split(key, 16)
    f32 = jnp.float32
    nrm = lambda k, shape, s: jax.random.normal(k, shape, f32) * s
    gain = lambda k, shape: 1.0 + 0.02 * jax.random.normal(k, shape, f32)
    return {
        "x": nrm(ks[0], (BATCH, SEQ, D_MODEL), 1.0),
        "mix_norm_g": gain(ks[1], (DEPTH, D_MODEL)),
        "w_in": nrm(ks[2], (DEPTH, D_MODEL, IN_COLS), D_MODEL ** -0.5),
        "q_norm_g": gain(ks[3], (DEPTH, HEAD_DIM)),
        "k_norm_g": gain(ks[4], (DEPTH, HEAD_DIM)),
        "sgu_v_norm_g": gain(ks[5], (DEPTH, SGU_WIDTH)),
        "sgu_w": nrm(ks[6], (DEPTH, SGU_GROUPS, CHUNK, CHUNK), CHUNK ** -0.5),
        "sgu_b": 1.0 + 0.1 * jax.random.normal(ks[7], (DEPTH, SGU_GROUPS, CHUNK), f32),
        "sb_out_norm_g": gain(ks[8], (DEPTH, SB_WIDTH)),
        "sgu_out_norm_g": gain(ks[9], (DEPTH, SGU_WIDTH)),
        "w_out": nrm(ks[10], (DEPTH, MIX_WIDTH, D_MODEL), MIX_WIDTH ** -0.5),
        "ffn_norm_g": gain(ks[11], (DEPTH, D_MODEL)),
        "peer_w_q": nrm(ks[12], (DEPTH, D_MODEL, PEER_HEADS * PEER_QDIM), D_MODEL ** -0.5),
        "peer_sub_keys": nrm(ks[13], (DEPTH, PEER_HEADS, 2, PEER_NKEYS, PEER_HALF), PEER_HALF ** -0.5),
        "peer_u": nrm(ks[14], (DEPTH, PEER_EXPERTS, D_MODEL), D_MODEL ** -0.5),
        "peer_v": nrm(ks[15], (DEPTH, PEER_EXPERTS, D_MODEL), PEER_HEADS ** -0.5),
    }


def reference(x, mix_norm_g, w_in, q_norm_g, k_norm_g, sgu_v_norm_g, sgu_w, sgu_b,
              sb_out_norm_g, sgu_out_norm_g, w_out, ffn_norm_g, peer_w_q, peer_sub_keys,
              peer_u, peer_v):
    B, S, _ = x.shape
    h = x
    for l in range(DEPTH):
        hn = rms_norm(h, mix_norm_g[l])
        proj = hn @ w_in[l]
        q, k, v, u_s, v_s = jnp.split(
            proj, [SB_WIDTH, 2 * SB_WIDTH, 3 * SB_WIDTH, 3 * SB_WIDTH + SGU_WIDTH], axis=-1)
        to_heads = lambda a: a.reshape(B, S, SB_HEADS, HEAD_DIM).transpose(0, 2, 1, 3)
        qh = rms_norm(to_heads(q), q_norm_g[l])
        kh = rms_norm(to_heads(k), k_norm_g[l])
        sb = stick_breaking_attention(qh, kh, to_heads(v))
        sb = sb.transpose(0, 2, 1, 3).reshape(B, S, SB_WIDTH).astype(x.dtype)
        sgu = spatial_gating(jax.nn.gelu(u_s), jax.nn.gelu(v_s),
                             sgu_v_norm_g[l], sgu_w[l], sgu_b[l])
        mixed = jnp.concatenate([rms_norm(sb, sb_out_norm_g[l]),
                                 rms_norm(sgu, sgu_out_norm_g[l])], axis=-1)
        h = h + mixed @ w_out[l]
        h = h + peer_ffn(rms_norm(h, ffn_norm_g[l]), peer_w_q[l], peer_sub_keys[l],
                         peer_u[l], peer_v[l])
    return h
```

```python
import functools

import jax
import jax.numpy as jnp
from jax import lax
from jax.experimental import pallas as pl
from jax.experimental.pallas import tpu as pltpu

EPS = 1e-6
LANES = 128
VMEM_LIMIT_BYTES = 60 * 1024 * 1024
SB_EXIT_LOG = -104.0

F32 = jnp.float32
BF16 = jnp.bfloat16


def _params(*sem):
    return pltpu.CompilerParams(dimension_semantics=sem, vmem_limit_bytes=VMEM_LIMIT_BYTES)


def _tile(n, want):
    return want if n % want == 0 else n


def _rms_bf16_kernel(x_ref, g_ref, o_ref):
    x = x_ref[...]
    ms = jnp.mean(x * x, axis=-1, keepdims=True)
    o_ref[...] = (x * lax.rsqrt(ms + EPS) * g_ref[...]).astype(o_ref.dtype)


def rms_to_bf16(x, g):
    m, d = x.shape
    tm = _tile(m, 256)
    return pl.pallas_call(
        _rms_bf16_kernel,
        grid=(m // tm,),
        in_specs=[pl.BlockSpec((tm, d), lambda i: (i, 0)),
                  pl.BlockSpec((1, d), lambda i: (0, 0))],
        out_specs=pl.BlockSpec((tm, d), lambda i: (i, 0)),
        out_shape=jax.ShapeDtypeStruct((m, d), BF16),
        compiler_params=_params("parallel"),
        name="rms_to_bf16",
    )(x, g.reshape(1, d))


def _qkv_kernel(x_ref, w_ref, g_ref, o_ref, *, sec_tiles):
    j = pl.program_id(1)
    acc = jnp.dot(x_ref[...], w_ref[...], preferred_element_type=F32)
    tn = acc.shape[1]

    @pl.when(j < 2 * sec_tiles)
    def _():
        g = g_ref[0]
        for s in range(tn // LANES):
            y = acc[:, s * LANES:(s + 1) * LANES]
            ms = jnp.mean(y * y, axis=-1, keepdims=True)
            o_ref[:, s * LANES:(s + 1) * LANES] = (y * lax.rsqrt(ms + EPS) * g).astype(o_ref.dtype)

    @pl.when(j >= 2 * sec_tiles)
    def _():
        o_ref[...] = acc.astype(o_ref.dtype)


def in_proj_qkv(xn, w_qkv, qk_gain):
    m, d = xn.shape
    n = w_qkv.shape[1]
    width = n // 3
    tm = _tile(m, 1024)
    tn = _tile(width, 512)
    sec_tiles = width // tn
    return pl.pallas_call(
        functools.partial(_qkv_kernel, sec_tiles=sec_tiles),
        grid=(m // tm, n // tn),
        in_specs=[pl.BlockSpec((tm, d), lambda i, j: (i, 0)),
                  pl.BlockSpec((d, tn), lambda i, j: (0, j)),
                  pl.BlockSpec((1, 1, LANES), lambda i, j: (jnp.minimum(j // sec_tiles, 1), 0, 0))],
        out_specs=pl.BlockSpec((tm, tn), lambda i, j: (i, j)),
        out_shape=jax.ShapeDtypeStruct((m, n), BF16),
        compiler_params=_params("parallel", "arbitrary"),
        name="in_proj_qkv",
    )(xn, w_qkv, qk_gain)


def _uv_kernel(x_ref, w_ref, o_ref):
    acc = jnp.dot(x_ref[...], w_ref[...], preferred_element_type=F32)
    o_ref[...] = jax.nn.gelu(acc)


def in_proj_uv(xn, w_uv):
    m, d = xn.shape
    n = w_uv.shape[1]
    tm = _tile(m, 1024)
    tn = _tile(n, 512)
    return pl.pallas_call(
        _uv_kernel,
        grid=(m // tm, n // tn),
        in_specs=[pl.BlockSpec((tm, d), lambda i, j: (i, 0)),
                  pl.BlockSpec((d, tn), lambda i, j: (0, j))],
        out_specs=pl.BlockSpec((tm, tn), lambda i, j: (i, j)),
        out_shape=jax.ShapeDtypeStruct((m, n), F32),
        compiler_params=_params("parallel", "arbitrary"),
        name="in_proj_uv",
    )(xn, w_uv)


def _attn_kernel(q_ref, k_ref, v_ref, m_ref, o_ref, acc_ref, carry_ref, *, hg, scale):
    i = pl.program_id(1)
    tq = q_ref.shape[0]
    cum = m_ref[...]
    row = lax.broadcasted_iota(jnp.int32, (tq, tq), 0)
    col = lax.broadcasted_iota(jnp.int32, (tq, tq), 1)
    causal = col < row

    def visit(j, hh, masked):
        hs = slice(hh * LANES, (hh + 1) * LANES)
        rows = pl.ds(pl.multiple_of(j * tq, tq), tq)
        q = q_ref[:, hs]
        k = k_ref[rows, hs]
        v = v_ref[rows, hs]
        z = lax.dot_general(q, k, (((1,), (1,)), ((), ())), preferred_element_type=F32) * scale
        log1m = -(jnp.maximum(z, 0.0) + jnp.log1p(jnp.exp(-jnp.abs(z))))
        if masked:
            log1m = jnp.where(causal, log1m, 0.0)
        hi = log1m.astype(BF16)
        lo = (log1m - hi.astype(F32)).astype(BF16)
        sums = (jnp.dot(hi, cum, preferred_element_type=F32)
                + jnp.dot(lo, cum, preferred_element_type=F32))
        carry = carry_ref[:, hs]
        a = jnp.exp(z + sums[:, :tq] + carry)
        if masked:
            a = jnp.where(causal, a, 0.0)
        acc_ref[:, hs] += jnp.dot(a.astype(BF16), v, preferred_element_type=F32)
        carry = carry + sums[:, tq:]
        carry_ref[:, hs] = carry
        return jnp.max(carry)

    acc_ref[...] = jnp.zeros_like(acc_ref)
    carry_ref[...] = jnp.zeros_like(carry_ref)
    top = visit(i, 0, True)
    for hh in range(1, hg):
        top = jnp.maximum(top, visit(i, hh, True))

    def cond(c):
        j, top = c
        return jnp.logical_and(j >= 0, top > SB_EXIT_LOG)

    def body(c):
        j, _ = c
        top = visit(j, 0, False)
        for hh in range(1, hg):
            top = jnp.maximum(top, visit(j, hh, False))
        return j - 1, top

    lax.while_loop(cond, body, (i - 1, top))
    o_ref[...] = acc_ref[...]


def sb_attention(qkv, heads):
    s = qkv.shape[0]
    tq = LANES
    hg = 4 if heads % 4 == 0 else 1
    ng = heads // hg
    wg = hg * LANES
    cum = jnp.concatenate(
        [(jnp.arange(tq)[:, None] >= jnp.arange(tq)[None, :]).astype(BF16),
         jnp.ones((tq, LANES), BF16)], axis=1)
    return pl.pallas_call(
        functools.partial(_attn_kernel, hg=hg, scale=float(LANES) ** -0.5),
        grid=(ng, s // tq),
        in_specs=[pl.BlockSpec((tq, wg), lambda g, i: (i, g)),
                  pl.BlockSpec((s, wg), lambda g, i: (0, ng + g)),
                  pl.BlockSpec((s, wg), lambda g, i: (0, 2 * ng + g)),
                  pl.BlockSpec((tq, tq + LANES), lambda g, i: (0, 0))],
        out_specs=pl.BlockSpec((tq, wg), lambda g, i: (i, g)),
        out_shape=jax.ShapeDtypeStruct((s, heads * LANES), F32),
        scratch_shapes=[pltpu.VMEM((tq, wg), F32), pltpu.VMEM((tq, wg), F32)],
        compiler_params=_params("parallel", "arbitrary"),
        name="sb_attention",
    )(qkv, qkv, qkv, cum)


def _mix_kernel(sb_ref, uv_ref, w_ref, b_ref, vg_ref, sbg_ref, sgg_ref, o_ref, sg_ref, *, groups):
    w_sb = sb_ref.shape[1]
    w_sg = groups * LANES
    sb = sb_ref[...]
    ms = jnp.mean(sb * sb, axis=-1, keepdims=True)
    o_ref[:, :w_sb] = (sb * lax.rsqrt(ms + EPS) * sbg_ref[...]).astype(o_ref.dtype)

    row = lax.broadcasted_iota(jnp.int32, (LANES, LANES), 0)
    col = lax.broadcasted_iota(jnp.int32, (LANES, LANES), 1)
    tril = col <= row
    ssq = jnp.zeros((LANES, 1), F32)
    for g in range(groups):
        gs = slice(g * LANES, (g + 1) * LANES)
        v = uv_ref[:, w_sg + g * LANES:w_sg + (g + 1) * LANES]
        vn = v * lax.rsqrt(jnp.mean(v * v, axis=-1, keepdims=True) + EPS) * vg_ref[:, gs]
        w = jnp.where(tril, w_ref[g], 0.0).astype(BF16)
        mixed = jnp.dot(w, vn.astype(BF16), preferred_element_type=F32) + b_ref[g]
        out = uv_ref[:, gs] * mixed
        sg_ref[:, gs] = out
        ssq = ssq + jnp.sum(out * out, axis=-1, keepdims=True)
    inv = lax.rsqrt(ssq / w_sg + EPS)
    o_ref[:, w_sb:] = (sg_ref[...] * inv * sgg_ref[...]).astype(o_ref.dtype)


def mix_norm(sb, uv, sgu_w, sgu_b, v_gain, sb_gain, sg_gain):
    s, w_sb = sb.shape
    groups = sgu_w.shape[0]
    w_sg = groups * LANES
    bias = jnp.broadcast_to(sgu_b[:, :, None], (groups, LANES, LANES))
    const = lambda i: (0, 0)
    return pl.pallas_call(
        functools.partial(_mix_kernel, groups=groups),
        grid=(s // LANES,),
        in_specs=[pl.BlockSpec((LANES, w_sb), lambda i: (i, 0)),
                  pl.BlockSpec((LANES, 2 * w_sg), lambda i: (i, 0)),
                  pl.BlockSpec((groups, LANES, LANES), lambda i: (0, 0, 0)),
                  pl.BlockSpec((groups, LANES, LANES), lambda i: (0, 0, 0)),
                  pl.BlockSpec((1, w_sg), const),
                  pl.BlockSpec((1, w_sb), const),
                  pl.BlockSpec((1, w_sg), const)],
        out_specs=pl.BlockSpec((LANES, w_sb + w_sg), lambda i: (i, 0)),
        out_shape=jax.ShapeDtypeStruct((s, w_sb + w_sg), BF16),
        scratch_shapes=[pltpu.VMEM((LANES, w_sg), F32)],
        compiler_params=_params("parallel"),
        name="mix_norm",
    )(sb, uv, sgu_w, bias, v_gain.reshape(1, w_sg), sb_gain.reshape(1, w_sb), sg_gain.reshape(1, w_sg))


def _out_proj_kernel(a_ref, w_ref, x_ref, o_ref):
    o_ref[...] = x_ref[...] + jnp.dot(a_ref[...], w_ref[...], preferred_element_type=F32)


def out_proj(mixed, w_out, x):
    m, k = mixed.shape
    n = w_out.shape[1]
    tm = _tile(m, 1024)
    tn = _tile(n, 512)
    return pl.pallas_call(
        _out_proj_kernel,
        grid=(m // tm, n // tn),
        in_specs=[pl.BlockSpec((tm, k), lambda i, j: (i, 0)),
                  pl.BlockSpec((k, tn), lambda i, j: (0, j)),
                  pl.BlockSpec((tm, tn), lambda i, j: (i, j))],
        out_specs=pl.BlockSpec((tm, tn), lambda i, j: (i, j)),
        out_shape=jax.ShapeDtypeStruct((m, n), F32),
        compiler_params=_params("parallel", "arbitrary"),
        name="out_proj",
    )(mixed, w_out, x)


def _ffn_norm_kernel(h_ref, g_ref, t_ref, tt_ref):
    h = h_ref[...]
    ms = jnp.mean(h * h, axis=-1, keepdims=True)
    y = h * lax.rsqrt(ms + EPS) * g_ref[...]
    t_ref[...] = y.astype(t_ref.dtype)
    tt_ref[...] = y.T.astype(tt_ref.dtype)


def ffn_norm(h, g):
    m, d = h.shape
    tm = _tile(m, 256)
    return pl.pallas_call(
        _ffn_norm_kernel,
        grid=(m // tm,),
        in_specs=[pl.BlockSpec((tm, d), lambda i: (i, 0)),
                  pl.BlockSpec((1, d), lambda i: (0, 0))],
        out_specs=[pl.BlockSpec((tm, d), lambda i: (i, 0)),
                   pl.BlockSpec((d, tm), lambda i: (0, i))],
        out_shape=[jax.ShapeDtypeStruct((m, d), BF16), jax.ShapeDtypeStruct((d, m), BF16)],
        compiler_params=_params("parallel"),
        name="ffn_norm",
    )(h, g.reshape(1, d))


def _scores_kernel(t_ref, w_ref, k_ref, o_ref):
    q = jnp.dot(t_ref[...], w_ref[...], preferred_element_type=F32).astype(BF16)
    half = q.shape[1] // 2
    for p in range(2):
        o_ref[0, p] = lax.dot_general(k_ref[0, p], q[:, p * half:(p + 1) * half],
                                      (((1,), (1,)), ((), ())), preferred_element_type=F32)


def peer_scores(tn, w_q, sub_keys):
    t, d = tn.shape
    heads, _, nk, half = sub_keys.shape
    tm = _tile(t, 1024)
    return pl.pallas_call(
        _scores_kernel,
        grid=(t // tm, heads),
        in_specs=[pl.BlockSpec((tm, d), lambda i, h: (i, 0)),
                  pl.BlockSpec((d, 2 * half), lambda i, h: (0, h)),
                  pl.BlockSpec((1, 2, nk, half), lambda i, h: (h, 0, 0, 0))],
        out_specs=pl.BlockSpec((1, 2, nk, tm), lambda i, h: (h, 0, 0, i)),
        out_shape=jax.ShapeDtypeStruct((heads, 2, nk, t), F32),
        compiler_params=_params("parallel", "arbitrary"),
        name="peer_scores",
    )(tn, w_q, sub_keys)


def _extract_top(x, n, store):
    rows = lax.broadcasted_iota(jnp.int32, x.shape, 0).astype(F32)
    for r in range(n):
        m = jnp.max(x, axis=0, keepdims=True)
        first = jnp.min(jnp.where(x == m, rows, 1e9), axis=0, keepdims=True)
        x = jnp.where(rows == first, -jnp.inf, x)
        store(r, m)


def _topk_kernel(s_ref, e1_ref, e2_ref, tau_ref, top_ref, cand_ref, best_ref, *, topk):
    s1 = s_ref[0, 0]
    s2 = s_ref[0, 1]

    def store_top(p):
        def store(r, m):
            top_ref[p, r:r + 1, :] = m
        return store

    _extract_top(s1, topk, store_top(0))
    _extract_top(s2, topk, store_top(1))
    a = top_ref[0]
    b = top_ref[1]
    cand_ref[0:topk, :] = a + b[0:1, :]
    for j in range(1, 8):
        cand_ref[topk + (j - 1) * 8:topk + j * 8, :] = a[0:8, :] + b[j:j + 1, :]
    cand_ref[topk + 56:topk + 64, :] = b[8:16, :] + a[0:1, :]

    def store_best(r, m):
        best_ref[r:r + 1, :] = m

    _extract_top(cand_ref[...], topk, store_best)
    best = best_ref[...]
    top = best[0:1, :]
    z = jnp.sum(jnp.exp(best - top), axis=0, keepdims=True)
    tau_ref[0] = best[topk - 1:topk, :]
    e1_ref[0] = jnp.exp(s1 - a[0:1, :]) / z
    e2_ref[0] = jnp.exp(s2 - b[0:1, :])


def peer_topk(scores, topk):
    heads, _, nk, t = scores.shape
    assert topk == 16, "candidate layout is written for top-16"
    tl = _tile(t, 256)
    return pl.pallas_call(
        functools.partial(_topk_kernel, topk=topk),
        grid=(heads, t // tl),
        in_specs=[pl.BlockSpec((1, 2, nk, tl), lambda h, i: (h, 0, 0, i))],
        out_specs=[pl.BlockSpec((1, nk, tl), lambda h, i: (h, 0, i)),
                   pl.BlockSpec((1, nk, tl), lambda h, i: (h, 0, i)),
                   pl.BlockSpec((1, 1, tl), lambda h, i: (h, 0, i))],
        out_shape=[jax.ShapeDtypeStruct((heads, nk, t), F32),
                   jax.ShapeDtypeStruct((heads, nk, t), F32),
                   jax.ShapeDtypeStruct((heads, 1, t), F32)],
        scratch_shapes=[pltpu.VMEM((2, topk, tl), F32),
                        pltpu.VMEM((topk + 64, tl), F32),
                        pltpu.VMEM((topk, tl), F32)],
        compiler_params=_params("parallel", "arbitrary"),
        name="peer_topk",
    )(scores)


def _experts_kernel(tt_ref, u_ref, v_ref, s_ref, e1_ref, e2_ref, tau_ref, o_ref, coef_ref, *, heads):
    e = pl.program_id(1)
    te = u_ref.shape[0]
    nk = s_ref.shape[2]

    @pl.when(e == 0)
    def _():
        o_ref[...] = jnp.zeros_like(o_ref)

    act = jax.nn.gelu(jnp.dot(u_ref[...], tt_ref[...], preferred_element_type=F32))
    for ii in range(te // nk):
        i1 = e * (te // nk) + ii
        g = None
        for h in range(heads):
            s1 = s_ref[h, 0, pl.ds(i1, 1), :]
            w1 = e1_ref[h, pl.ds(i1, 1), :]
            hit = (s_ref[h, 1] + s1) >= tau_ref[h]
            term = jnp.where(hit, e2_ref[h] * w1, 0.0)
            g = term if g is None else g + term
        coef_ref[ii * nk:(ii + 1) * nk, :] = g * act[ii * nk:(ii + 1) * nk, :]
    coef = coef_ref[...].T.astype(BF16)
    o_ref[...] += jnp.dot(coef, v_ref[...], preferred_element_type=F32)


def peer_experts(tn_t, u_tab, v_tab, scores, e1, e2, tau):
    d, t = tn_t.shape
    n_exp = u_tab.shape[0]
    heads, _, nk, _ = scores.shape
    tm = _tile(t, 512)
    te = _tile(n_exp, 256)
    return pl.pallas_call(
        functools.partial(_experts_kernel, heads=heads),
        grid=(t // tm, n_exp // te),
        in_specs=[pl.BlockSpec((d, tm), lambda i, e: (0, i)),
                  pl.BlockSpec((te, d), lambda i, e: (e, 0)),
                  pl.BlockSpec((te, d), lambda i, e: (e, 0)),
                  pl.BlockSpec((heads, 2, nk, tm), lambda i, e: (0, 0, 0, i)),
                  pl.BlockSpec((heads, nk, tm), lambda i, e: (0, 0, i)),
                  pl.BlockSpec((heads, nk, tm), lambda i, e: (0, 0, i)),
                  pl.BlockSpec((heads, 1, tm), lambda i, e: (0, 0, i))],
        out_specs=pl.BlockSpec((tm, d), lambda i, e: (i, 0)),
        out_shape=jax.ShapeDtypeStruct((t, d), F32),
        scratch_shapes=[pltpu.VMEM((te, tm), F32)],
        compiler_params=_params("parallel", "arbitrary"),
        name="peer_experts",
    )(tn_t, u_tab, v_tab, scores, e1, e2, tau)


def _add_kernel(a_ref, b_ref, o_ref):
    o_ref[...] = a_ref[...] + b_ref[...]


def residual_add(a, b):
    m, d = a.shape
    tm = _tile(m, 256)
    spec = pl.BlockSpec((tm, d), lambda i: (i, 0))
    return pl.pallas_call(
        _add_kernel, grid=(m // tm,), in_specs=[spec, spec], out_specs=spec,
        out_shape=jax.ShapeDtypeStruct((m, d), a.dtype),
        compiler_params=_params("parallel"), name="residual_add",
    )(a, b)


def _layer(h, mix_g, w_in, q_g, k_g, sgu_vg, sgu_w, sgu_b, sb_g, sg_g, w_out, ffn_g,
           w_q, sub_keys, u_tab, v_tab, topk):
    groups = sgu_w.shape[0]
    w_sg = groups * LANES
    w_sb = (w_in.shape[1] - 2 * w_sg) // 3
    heads = w_sb // LANES

    xn = rms_to_bf16(h, mix_g)
    qkv = in_proj_qkv(xn, w_in[:, :3 * w_sb].astype(BF16), jnp.stack([q_g, k_g]).reshape(2, 1, LANES))
    uv = in_proj_uv(xn, w_in[:, 3 * w_sb:].astype(BF16))
    sb = sb_attention(qkv, heads)
    mixed = mix_norm(sb, uv, sgu_w, sgu_b, sgu_vg, sb_g, sg_g)
    h = out_proj(mixed, w_out.astype(BF16), h)

    tn, tn_t = ffn_norm(h, ffn_g)
    scores = peer_scores(tn, w_q.astype(BF16), sub_keys.astype(BF16))
    e1, e2, tau = peer_topk(scores, topk)
    peer = peer_experts(tn_t, u_tab.astype(BF16), v_tab.astype(BF16), scores, e1, e2, tau)
    return residual_add(h, peer)


def kernel(x, mix_norm_g, w_in, q_norm_g, k_norm_g, sgu_v_norm_g, sgu_w, sgu_b, sb_out_norm_g,
           sgu_out_norm_g, w_out, ffn_norm_g, peer_w_q, peer_sub_keys, peer_u, peer_v):
    b, s, d = x.shape
    assert b == 1, "tokens of different batch rows must not mix; only batch 1 is laid out here"
    h = x.reshape(s, d)
    for l in range(mix_norm_g.shape[0]):
        h = _layer(h, mix_norm_g[l], w_in[l], q_norm_g[l], k_norm_g[l], sgu_v_norm_g[l], sgu_w[l],
                   sgu_b[l], sb_out_norm_g[l], sgu_out_norm_g[l], w_out[l], ffn_norm_g[l],
                   peer_w_q[l], peer_sub_keys[l], peer_u[l], peer_v[l], topk=16)
    return h.reshape(b, s, d)
```
